```python
import jax, jax.numpy as jnp
from jax import lax
import numpy as np

D_MODEL = 2048
BATCH = 2
SEQ = 4096
DEPTH = 4
DEC_BATCH = 8
DEC_SEQ = 1
PAST_LEN = 16384
PAGE_SIZE = 128

N_MIXERS = 2
N_ATTN = (DEPTH + N_MIXERS - 1) // N_MIXERS
N_RWKV = DEPTH // N_MIXERS
N_VRES = max(N_RWKV - 1, 0)
N_HEADS = 16
HEAD_DIM = D_MODEL // N_HEADS
MOBA_BLOCK = 256
MOBA_TOPK = 3
QUERY_CHUNK = 16
RWKV_HEAD_DIM = 64
RWKV_HEADS = D_MODEL // RWKV_HEAD_DIM
D_DECAY_LORA = 96
D_AAA_LORA = 96
D_MV_LORA = 64
D_GATE_LORA = 256
D_FF = ((8 * D_MODEL // 3 + 255) // 256) * 256
D_PLE = 256
RMS_EPS = 1e-6
GN_EPS = 64e-5

kernel_name = 'moba_rwkv7_macaron_ple_step'


def rms_norm(x, g):
    xf = x.astype(jnp.float32)
    y = xf * lax.rsqrt(jnp.mean(xf * xf, axis=-1, keepdims=True) + RMS_EPS)
    return y.astype(x.dtype) * g


def swiglu(x, wi, wo):
    gate, up = jnp.split(x @ wi, 2, axis=-1)
    return (jax.nn.silu(gate) * up) @ wo


def alibi_slopes(n_heads):
    return 2.0 ** (-8.0 * (jnp.arange(n_heads, dtype=jnp.float32) + 1.0) / n_heads)


def moba_attend(q, k_all, v_all, offset):
    b, tq, h, hd = q.shape
    l = k_all.shape[1]
    nb = -(-l // MOBA_BLOCK)
    pad = nb * MOBA_BLOCK - l
    k_blk = jnp.pad(k_all, ((0, 0), (0, pad), (0, 0), (0, 0))).reshape(b, nb, MOBA_BLOCK, h, hd)
    v_blk = jnp.pad(v_all, ((0, 0), (0, pad), (0, 0), (0, 0))).reshape(b, nb, MOBA_BLOCK, h, hd)
    k_mean = jnp.mean(k_blk, axis=2, dtype=jnp.float32).astype(q.dtype)
    n_top = min(MOBA_TOPK, nb)
    qc = min(QUERY_CHUNK, tq)
    n_c = -(-tq // qc)
    q_pad = jnp.pad(q, ((0, 0), (0, n_c * qc - tq), (0, 0), (0, 0)))
    q_chunks = jnp.moveaxis(q_pad.reshape(b, n_c, qc, h, hd), 1, 0)
    pos = jnp.minimum(offset + jnp.arange(n_c * qc, dtype=jnp.int32), l - 1).reshape(n_c, qc)
    slopes = alibi_slopes(h)
    b_ix = jnp.arange(b)[:, None, None, None]
    h_ix = jnp.arange(h)[None, :, None, None]
    blk_ids = jnp.arange(nb, dtype=jnp.int32)
    offs = jnp.arange(MOBA_BLOCK, dtype=jnp.int32)
    scale = hd ** -0.5

    def chunk(args):
        qb, pb = args
        own = pb // MOBA_BLOCK
        gate = jnp.einsum('bqhd,bnhd->bhqn', qb, k_mean).astype(jnp.float32)
        gate = jnp.where(blk_ids < own[:, None], gate, -jnp.inf)
        _, top = lax.top_k(gate, n_top)
        top_ok = jnp.arange(n_top) < own[:, None]
        idx = jnp.concatenate([top.astype(jnp.int32), jnp.broadcast_to(own[None, None, :, None], (b, h, qc, 1))], axis=-1)
        ok = jnp.concatenate([jnp.broadcast_to(top_ok, (b, h, qc, n_top)), jnp.ones((b, h, qc, 1), bool)], axis=-1)
        k_sel = k_blk[b_ix, idx, :, h_ix, :]
        v_sel = v_blk[b_ix, idx, :, h_ix, :]
        s = jnp.einsum('bqhd,bhqnsd->bhqns', qb, k_sel).astype(jnp.float32) * scale
        dist = pb[:, None, None] - (idx[..., None] * MOBA_BLOCK + offs)
        mask = ok[..., None] & (dist >= 0)
        s = jnp.where(mask, s - slopes[:, None, None, None] * dist.astype(jnp.float32), -jnp.inf)
        p = jax.nn.softmax(s.reshape(b, h, qc, -1), axis=-1).reshape(s.shape).astype(v_sel.dtype)
        return jnp.einsum('bhqns,bhqnsd->bqhd', p, v_sel)

    o = lax.map(chunk, (q_chunks, pos))
    return jnp.moveaxis(o, 0, 1).reshape(b, n_c * qc, h, hd)[:, :tq]


def moba_mixer(xn, w_qkv, q_g, k_g, w_o, past_kv, offset):
    b, t, _ = xn.shape
    qkv = (xn @ w_qkv).reshape(b, t, 3, N_HEADS, HEAD_DIM)
    q = rms_norm(qkv[:, :, 0], q_g)
    k = rms_norm(qkv[:, :, 1], k_g)
    v = qkv[:, :, 2]
    if past_kv is None:
        k_all, v_all = k, v
    else:
        k_all = jnp.concatenate([past_kv[0].astype(k.dtype), k], axis=1)
        v_all = jnp.concatenate([past_kv[1].astype(v.dtype), v], axis=1)
    o = moba_attend(q, k_all, v_all, offset)
    return o.reshape(b, t, D_MODEL) @ w_o, k, v


def wkv7_scan(s0, r, w, k, v, a, bb):
    def step(s, inp):
        r_t, w_t, k_t, v_t, a_t, b_t = inp
        sa = jnp.einsum('bhij,bhj->bhi', s, a_t)
        s = s * w_t[:, :, None, :] + sa[..., None] * b_t[:, :, None, :] + v_t[..., None] * k_t[:, :, None, :]
        return s, jnp.einsum('bhij,bhj->bhi', s, r_t)
    xs = tuple(jnp.moveaxis(z, 1, 0) for z in (r, w, k, v, a, bb))
    s, ys = lax.scan(step, s0, xs)
    return s, jnp.moveaxis(ys, 0, 1)


def rwkv7_time_mix(xn, shift_prev, s0, v_first, mix, w_rkv, w0, w1, w2, a0, a1, a2, g1, g2,
                   k_k, k_a, r_k, ln_w, ln_b, w_o, vres):
    b, t, d = xn.shape
    f32 = jnp.float32

    def heads(z):
        return z.reshape(b, t, RWKV_HEADS, RWKV_HEAD_DIM)

    x_prev = jnp.concatenate([shift_prev[:, None, :].astype(xn.dtype), xn[:, :-1]], axis=1)
    xm = xn[:, :, None, :] + (x_prev - xn)[:, :, None, :] * mix
    rkv = jnp.einsum('btjd,jde->btje', xm[:, :, :3], w_rkv)
    r, k, v = rkv[:, :, 0], rkv[:, :, 1], rkv[:, :, 2]
    xv, xw, xa, xg = xm[:, :, 2], xm[:, :, 3], xm[:, :, 4], xm[:, :, 5]
    w_log = -jax.nn.softplus(-(w0 + jnp.tanh(xw @ w1) @ w2).astype(f32)) - 0.5
    decay = jnp.exp(-jnp.exp(w_log))
    if vres is None:
        v_first = v
    else:
        v0, v1, v2 = vres
        v = v + (v_first - v) * jax.nn.sigmoid(v0 + (xv @ v1) @ v2)
    a = jax.nn.sigmoid(a0 + (xa @ a1) @ a2)
    g = jax.nn.sigmoid(xg @ g1) @ g2
    kk = heads((k * k_k).astype(f32))
    kk = kk / jnp.maximum(jnp.sqrt(jnp.sum(kk * kk, axis=-1, keepdims=True)), 1e-12)
    k = k * (1.0 + (a - 1.0) * k_a)
    rh, kh, vh, ah = heads(r.astype(f32)), heads(k.astype(f32)), heads(v.astype(f32)), heads(a.astype(f32))
    s, y = wkv7_scan(s0.astype(f32), rh, heads(decay), kh, vh, -kk, kk * ah)
    mu = jnp.mean(y, axis=-1, keepdims=True)
    var = jnp.mean((y - mu) ** 2, axis=-1, keepdims=True)
    y = (y - mu) * lax.rsqrt(var + GN_EPS)
    y = y * ln_w.reshape(RWKV_HEADS, RWKV_HEAD_DIM).astype(f32) + ln_b.reshape(RWKV_HEADS, RWKV_HEAD_DIM).astype(f32)
    y = y + jnp.sum(rh * kh * r_k.astype(f32), axis=-1, keepdims=True) * vh
    out = (y.reshape(b, t, d).astype(xn.dtype) * g) @ w_o
    return out, v_first, s, xn[:, -1]


def trunk(x, p, W, past_kv, wkv0, shift0, offset):
    new_k, new_v, new_wkv, new_shift = [], [], [], []
    v_first = None
    ia = 0
    ir = 0
    for i in range(DEPTH):
        g = W['norm_g'][i]
        x = x + 0.5 * swiglu(rms_norm(x, g[0]), W['ffa_wi'][i], W['ffa_wo'][i])
        xn = rms_norm(x, g[1])
        if i % N_MIXERS == 0:
            kv = None if past_kv is None else past_kv(ia)
            o, k_rows, v_rows = moba_mixer(xn, W['attn_w_qkv'][ia], W['attn_q_g'][ia], W['attn_k_g'][ia],
                                           W['attn_w_o'][ia], kv, offset)
            new_k.append(k_rows)
            new_v.append(v_rows)
            ia += 1
        else:
            vres = None if ir == 0 else (W['vres_v0'][ir - 1], W['vres_v1'][ir - 1], W['vres_v2'][ir - 1])
            o, v_first, s, last = rwkv7_time_mix(
                xn, shift0[ir], wkv0[ir], v_first, W['rwkv_mix'][ir], W['rwkv_w_rkv'][ir],
                W['rwkv_w0'][ir], W['rwkv_w1'][ir], W['rwkv_w2'][ir],
                W['rwkv_a0'][ir], W['rwkv_a1'][ir], W['rwkv_a2'][ir],
                W['rwkv_g1'][ir], W['rwkv_g2'][ir], W['rwkv_k_k'][ir], W['rwkv_k_a'][ir],
                W['rwkv_r_k'][ir], W['rwkv_ln_w'][ir], W['rwkv_ln_b'][ir], W['rwkv_w_o'][ir], vres)
            new_wkv.append(s)
            new_shift.append(last)
            ir += 1
        x = x + o
        x = x + 0.5 * swiglu(rms_norm(x, g[2]), W['ffb_wi'][i], W['ffb_wo'][i])
        gate = jax.nn.sigmoid(rms_norm(x, g[3]) @ W['ple_w_gate'][i])
        x = x + gate * (p[i] @ W['ple_w_in'][i])
    return x, jnp.stack(new_k), jnp.stack(new_v), jnp.stack(new_wkv), jnp.stack(new_shift)


def setup_inputs(seed: int = 0) -> dict:
    key = jax.random.key(seed)
    ks = iter(jax.random.split(key, 48))
    f32 = jnp.float32

    def nrm(shape, scale=1.0):
        return scale * jax.random.normal(next(ks), shape, f32)

    def unif(shape, lo, hi):
        return jax.random.uniform(next(ks), shape, f32, lo, hi)

    n_pages = PAST_LEN // PAGE_SIZE
    n_pool = (DEC_BATCH * n_pages * 5) // 4
    D, F = D_MODEL, D_FF
    out = {}
    out['x_prompt'] = nrm((BATCH, SEQ, D))
    out['x_sample'] = nrm((DEC_BATCH, DEC_SEQ, D))
    out['cache_k'] = nrm((N_ATTN, n_pool, PAGE_SIZE, N_HEADS, HEAD_DIM))
    out['cache_v'] = nrm((N_ATTN, n_pool, PAGE_SIZE, N_HEADS, HEAD_DIM))
    perm = jax.random.permutation(next(ks), n_pool)
    out['page_table'] = perm[:DEC_BATCH * n_pages].reshape(DEC_BATCH, n_pages).astype(jnp.int32)
    out['state_wkv'] = nrm((N_RWKV, DEC_BATCH, RWKV_HEADS, RWKV_HEAD_DIM, RWKV_HEAD_DIM), 0.5)
    out['state_shift'] = nrm((N_RWKV, DEC_BATCH, D))
    out['p_prompt'] = nrm((DEPTH, BATCH, SEQ, D_PLE))
    out['p_sample'] = nrm((DEPTH, DEC_BATCH, DEC_SEQ, D_PLE))
    out['norm_g'] = 1.0 + nrm((DEPTH, 4, D), 0.02)
    out['ffa_wi'] = nrm((DEPTH, D, 2 * F), D ** -0.5)
    out['ffa_wo'] = nrm((DEPTH, F, D), F ** -0.5)
    out['ffb_wi'] = nrm((DEPTH, D, 2 * F), D ** -0.5)
    out['ffb_wo'] = nrm((DEPTH, F, D), F ** -0.5)
    out['attn_w_qkv'] = nrm((N_ATTN, D, 3 * D), D ** -0.5)
    out['attn_q_g'] = 1.0 + nrm((N_ATTN, HEAD_DIM), 0.02)
    out['attn_k_g'] = 1.0 + nrm((N_ATTN, HEAD_DIM), 0.02)
    out['attn_w_o'] = nrm((N_ATTN, D, D), D ** -0.5)
    out['rwkv_mix'] = unif((N_RWKV, 6, D), 0.0, 1.0)
    out['rwkv_w_rkv'] = nrm((N_RWKV, 3, D, D), D ** -0.5)
    out['rwkv_w0'] = unif((N_RWKV, D), -6.5, -1.5)
    out['rwkv_w1'] = nrm((N_RWKV, D, D_DECAY_LORA), D ** -0.5)
    out['rwkv_w2'] = nrm((N_RWKV, D_DECAY_LORA, D), 0.1 * D_DECAY_LORA ** -0.5)
    out['rwkv_a0'] = nrm((N_RWKV, D), 0.1)
    out['rwkv_a1'] = nrm((N_RWKV, D, D_AAA_LORA), D ** -0.5)
    out['rwkv_a2'] = nrm((N_RWKV, D_AAA_LORA, D), 0.1 * D_AAA_LORA ** -0.5)
    out['rwkv_g1'] = nrm((N_RWKV, D, D_GATE_LORA), D ** -0.5)
    out['rwkv_g2'] = nrm((N_RWKV, D_GATE_LORA, D), D_GATE_LORA ** -0.5)
    out['rwkv_k_k'] = 0.85 + nrm((N_RWKV, D), 0.02)
    out['rwkv_k_a'] = 1.0 + nrm((N_RWKV, D), 0.02)
    out['rwkv_r_k'] = nrm((N_RWKV, RWKV_HEADS, RWKV_HEAD_DIM), 0.1)
    out['rwkv_ln_w'] = 1.0 + nrm((N_RWKV, D), 0.02)
    out['rwkv_ln_b'] = nrm((N_RWKV, D), 0.02)
    out['rwkv_w_o'] = nrm((N_RWKV, D, D), D ** -0.5)
    out['vres_v0'] = 1.0 + nrm((N_VRES, D), 0.1)
    out['vres_v1'] = nrm((N_VRES, D, D_MV_LORA), D ** -0.5)
    out['vres_v2'] = nrm((N_VRES, D_MV_LORA, D), 0.1 * D_MV_LORA ** -0.5)
    out['ple_w_in'] = nrm((DEPTH, D_PLE, D), D_PLE ** -0.5)
    out['ple_w_gate'] = nrm((DEPTH, D, D), D ** -0.5)
    return out


def reference(x_prompt, x_sample, cache_k, cache_v, page_table, state_wkv, state_shift, p_prompt, p_sample,
              norm_g, ffa_wi, ffa_wo, ffb_wi, ffb_wo, attn_w_qkv, attn_q_g, attn_k_g, attn_w_o,
              rwkv_mix, rwkv_w_rkv, rwkv_w0, rwkv_w1, rwkv_w2, rwkv_a0, rwkv_a1, rwkv_a2, rwkv_g1, rwkv_g2,
              rwkv_k_k, rwkv_k_a, rwkv_r_k, rwkv_ln_w, rwkv_ln_b, rwkv_w_o, vres_v0, vres_v1, vres_v2,
              ple_w_in, ple_w_gate):
    W = dict(norm_g=norm_g, ffa_wi=ffa_wi, ffa_wo=ffa_wo, ffb_wi=ffb_wi, ffb_wo=ffb_wo,
             attn_w_qkv=attn_w_qkv, attn_q_g=attn_q_g, attn_k_g=attn_k_g, attn_w_o=attn_w_o,
             rwkv_mix=rwkv_mix, rwkv_w_rkv=rwkv_w_rkv, rwkv_w0=rwkv_w0, rwkv_w1=rwkv_w1, rwkv_w2=rwkv_w2,
             rwkv_a0=rwkv_a0, rwkv_a1=rwkv_a1, rwkv_a2=rwkv_a2, rwkv_g1=rwkv_g1, rwkv_g2=rwkv_g2,
             rwkv_k_k=rwkv_k_k, rwkv_k_a=rwkv_k_a, rwkv_r_k=rwkv_r_k, rwkv_ln_w=rwkv_ln_w, rwkv_ln_b=rwkv_ln_b,
             rwkv_w_o=rwkv_w_o, vres_v0=vres_v0, vres_v1=vres_v1, vres_v2=vres_v2,
             ple_w_in=ple_w_in, ple_w_gate=ple_w_gate)
    b = x_prompt.shape[0]
    wkv0 = jnp.zeros((N_RWKV, b, RWKV_HEADS, RWKV_HEAD_DIM, RWKV_HEAD_DIM), jnp.float32)
    shift0 = jnp.zeros((N_RWKV, b, D_MODEL), x_prompt.dtype)
    y_prompt, k_prompt, v_prompt, wkv_prompt, shift_prompt = trunk(x_prompt, p_prompt, W, None, wkv0, shift0, 0)
    dec_b, n_pages = page_table.shape
    past_len = n_pages * PAGE_SIZE

    def sample_past(ia):
        kp = cache_k[ia, page_table].reshape(dec_b, past_len, N_HEADS, HEAD_DIM)
        vp = cache_v[ia, page_table].reshape(dec_b, past_len, N_HEADS, HEAD_DIM)
        return kp, vp

    y_sample, k_sample, v_sample, wkv_sample, shift_sample = trunk(
        x_sample, p_sample, W, sample_past, state_wkv, state_shift, past_len)
    return (y_prompt, y_sample, k_prompt, v_prompt, k_sample, v_sample,
            wkv_prompt, shift_prompt, wkv_sample, shift_sample)
```

```python
import functools
import math

import jax
import jax.numpy as jnp
from jax import lax
from jax.experimental import pallas as pl
from jax.experimental.pallas import tpu as pltpu

F32 = jnp.float32
BF16 = jnp.bfloat16
HIGHEST = lax.Precision.HIGHEST

N_HEADS = 16
HEAD_DIM = 128
MOBA_BLOCK = 256
MOBA_TOPK = 3
PAGE_SIZE = 128
RWKV_HEAD_DIM = 64
RMS_EPS = 1e-6
GN_EPS = 64e-5

LANES = 128
SUBLANES = 8
HEAD_GROUP = SUBLANES
VMEM_LIMIT_BYTES = 56 * 1024 * 1024

NT_DIMS = (((1,), (1,)), ((), ()))
TN_DIMS = (((0,), (0,)), ((), ()))


def _cparams(*sem):
    return pltpu.CompilerParams(dimension_semantics=sem, vmem_limit_bytes=VMEM_LIMIT_BYTES)


def _rms(x, g):
    ms = jnp.mean(x * x, axis=-1, keepdims=True)
    return x * lax.rsqrt(ms + RMS_EPS) * g


def _pick_tile(m, pref):
    t = min(m, pref)
    while m % t:
        t //= 2
    return t


def _ffn_kernel(x_ref, g_ref, wg_ref, wu_ref, wo_ref, o_ref, xn_ref, acc_ref):
    j = pl.program_id(1)

    @pl.when(j == 0)
    def _():
        xn_ref[...] = _rms(x_ref[...], g_ref[...]).astype(BF16)
        acc_ref[...] = jnp.zeros_like(acc_ref)

    xn = xn_ref[...]
    gate = jnp.dot(xn, wg_ref[...], preferred_element_type=F32)
    up = jnp.dot(xn, wu_ref[...], preferred_element_type=F32)
    h = (gate * jax.nn.sigmoid(gate) * up).astype(BF16)
    acc_ref[...] += jnp.dot(h, wo_ref[...], preferred_element_type=F32)

    @pl.when(j == pl.num_programs(1) - 1)
    def _():
        o_ref[...] = x_ref[...] + 0.5 * acc_ref[...]


def ffn(x, g, wi, wo, *, tm_pref=512, tf=512):
    m, d = x.shape
    f = wo.shape[0]
    tm = _pick_tile(m, tm_pref)
    nf = f // tf
    return pl.pallas_call(
        _ffn_kernel,
        grid=(m // tm, nf),
        in_specs=[
            pl.BlockSpec((tm, d), lambda i, j: (i, 0)),
            pl.BlockSpec((1, d), lambda i, j: (0, 0)),
            pl.BlockSpec((d, tf), lambda i, j: (0, j)),
            pl.BlockSpec((d, tf), lambda i, j: (0, j + nf)),
            pl.BlockSpec((tf, d), lambda i, j: (j, 0)),
        ],
        out_specs=pl.BlockSpec((tm, d), lambda i, j: (i, 0)),
        out_shape=jax.ShapeDtypeStruct((m, d), F32),
        scratch_shapes=[pltpu.VMEM((tm, d), BF16), pltpu.VMEM((tm, d), F32)],
        compiler_params=_cparams("parallel", "arbitrary"),
        name="ffn",
    )(x, g.reshape(1, d), wi, wi, wo)


def _ple_kernel(x_ref, xc_ref, g_ref, wg_ref, p_ref, wp_ref, o_ref, xn_ref):
    @pl.when(pl.program_id(1) == 0)
    def _():
        xn_ref[...] = _rms(x_ref[...], g_ref[...]).astype(BF16)

    gate = jax.nn.sigmoid(jnp.dot(xn_ref[...], wg_ref[...], preferred_element_type=F32))
    pin = jnp.dot(p_ref[...].astype(BF16), wp_ref[...], preferred_element_type=F32)
    o_ref[...] = xc_ref[...] + gate * pin


def ple(x, g, wg, p, wp, *, tm_pref=512, tn=512):
    m, d = x.shape
    dp = p.shape[1]
    tm = _pick_tile(m, tm_pref)
    return pl.pallas_call(
        _ple_kernel,
        grid=(m // tm, d // tn),
        in_specs=[
            pl.BlockSpec((tm, d), lambda i, j: (i, 0)),
            pl.BlockSpec((tm, tn), lambda i, j: (i, j)),
            pl.BlockSpec((1, d), lambda i, j: (0, 0)),
            pl.BlockSpec((d, tn), lambda i, j: (0, j)),
            pl.BlockSpec((tm, dp), lambda i, j: (i, 0)),
            pl.BlockSpec((dp, tn), lambda i, j: (0, j)),
        ],
        out_specs=pl.BlockSpec((tm, tn), lambda i, j: (i, j)),
        out_shape=jax.ShapeDtypeStruct((m, d), F32),
        scratch_shapes=[pltpu.VMEM((tm, d), BF16)],
        compiler_params=_cparams("parallel", "arbitrary"),
        name="ple",
    )(x, x, g.reshape(1, d), wg, p, wp)


def _qkv_kernel(x_ref, g_ref, w_ref, qg_ref, kg_ref, q_ref, k_ref, v_ref, xn_ref, *, n_sec):
    j = pl.program_id(1)

    @pl.when(j == 0)
    def _():
        xn_ref[...] = _rms(x_ref[...], g_ref[...]).astype(BF16)

    y = jnp.dot(xn_ref[...], w_ref[...], preferred_element_type=F32)
    heads = y.shape[1] // HEAD_DIM

    def head_norm(gain):
        parts = [_rms(y[:, i * HEAD_DIM:(i + 1) * HEAD_DIM], gain) for i in range(heads)]
        return jnp.concatenate(parts, axis=1)

    @pl.when(j < n_sec)
    def _():
        q_ref[...] = head_norm(qg_ref[...])

    @pl.when((j >= n_sec) & (j < 2 * n_sec))
    def _():
        k_ref[...] = head_norm(kg_ref[...])

    @pl.when(j >= 2 * n_sec)
    def _():
        v_ref[...] = y


def qkv_proj(x, g, w, qg, kg, *, tm_pref=512, tn=512):
    m, d = x.shape
    tm = _pick_tile(m, tm_pref)
    n_sec = d // tn
    out = jax.ShapeDtypeStruct((m, d), F32)
    sec_spec = lambda s: pl.BlockSpec(
        (tm, tn), lambda i, j: (i, jnp.clip(j - s * n_sec, 0, n_sec - 1)))
    return pl.pallas_call(
        functools.partial(_qkv_kernel, n_sec=n_sec),
        grid=(m // tm, 3 * n_sec),
        in_specs=[
            pl.BlockSpec((tm, d), lambda i, j: (i, 0)),
            pl.BlockSpec((1, d), lambda i, j: (0, 0)),
            pl.BlockSpec((d, tn), lambda i, j: (0, j)),
            pl.BlockSpec((1, HEAD_DIM), lambda i, j: (0, 0)),
            pl.BlockSpec((1, HEAD_DIM), lambda i, j: (0, 0)),
        ],
        out_specs=[sec_spec(0), sec_spec(1), sec_spec(2)],
        out_shape=[out, out, out],
        scratch_shapes=[pltpu.VMEM((tm, d), BF16)],
        compiler_params=_cparams("parallel", "arbitrary"),
        name="qkv_proj",
    )(x, g.reshape(1, d), w, qg.reshape(1, HEAD_DIM), kg.reshape(1, HEAD_DIM))


def _mm_res_kernel(a_ref, w_ref, r_ref, o_ref):
    o_ref[...] = r_ref[...] + jnp.dot(a_ref[...], w_ref[...], preferred_element_type=F32)


def mm_res(a, w, res, *, tm_pref=1024, tn=512):
    m, kd = a.shape
    n = w.shape[1]
    tm = _pick_tile(m, tm_pref)
    return pl.pallas_call(
        _mm_res_kernel,
        grid=(m // tm, n // tn),
        in_specs=[
            pl.BlockSpec((tm, kd), lambda i, j: (i, 0)),
            pl.BlockSpec((kd, tn), lambda i, j: (0, j)),
            pl.BlockSpec((tm, tn), lambda i, j: (i, j)),
        ],
        out_specs=pl.BlockSpec((tm, tn), lambda i, j: (i, j)),
        out_shape=jax.ShapeDtypeStruct((m, n), F32),
        compiler_params=_cparams("parallel", "arbitrary"),
        name="mm_res",
    )(a, w, res)


def _topk_lanes(gate, col, n_cols, n_top):
    picks = []
    for _ in range(n_top):
        m = jnp.max(gate, axis=-1, keepdims=True)
        idx = jnp.min(jnp.where(gate == m, col, n_cols), axis=-1, keepdims=True)
        picks.append(idx)
        gate = jnp.where(col == idx, -jnp.inf, gate)
    return picks


def _alibi_slope(h):
    return jnp.exp2(jnp.full((1, 1), -8.0 / N_HEADS, F32) * (h + 1).astype(F32))


def _moba_kernel(q_ref, k_ref, v_ref, o_ref, km_ref, *, nb):
    h = pl.program_id(1)
    qi = pl.program_id(2)
    blk = MOBA_BLOCK
    scale = HEAD_DIM ** -0.5

    @pl.when(qi == 0)
    def _():
        for n in range(nb):
            km_ref[n:n + 1, :] = jnp.sum(k_ref[n * blk:(n + 1) * blk, :], axis=0, keepdims=True) * (1.0 / blk)

    q = q_ref[...]
    gate = lax.dot_general(q, km_ref[...], NT_DIMS, precision=HIGHEST, preferred_element_type=F32)
    col = lax.broadcasted_iota(jnp.int32, gate.shape, 1)
    gate = jnp.where(col < qi, gate, -jnp.inf)
    picks = _topk_lanes(gate, col, nb, min(MOBA_TOPK, nb))
    sel = jnp.zeros(gate.shape, F32)
    for idx in picks:
        sel = jnp.where(col == jnp.where(idx < qi, idx, nb), 1.0, sel)

    qb = q.astype(BF16)
    slope = _alibi_slope(h)
    r_io = lax.broadcasted_iota(jnp.int32, (blk, blk), 0)
    c_io = lax.broadcasted_iota(jnp.int32, (blk, blk), 1)
    rel = (r_io - c_io).astype(F32)

    def attend(n, carry, mask):
        m, l, acc = carry
        start = pl.multiple_of(n * blk, blk)
        kb = k_ref[pl.ds(start, blk), :].astype(BF16)
        vb = v_ref[pl.ds(start, blk), :].astype(BF16)
        s = lax.dot_general(qb, kb, NT_DIMS, preferred_element_type=F32) * scale
        dist = rel + ((qi - n) * blk).astype(F32)
        s = jnp.where(mask, s - slope * dist, -jnp.inf)
        m_new = jnp.maximum(m, jnp.max(s, axis=-1, keepdims=True))
        alpha = jnp.exp(m - m_new)
        p = jnp.exp(s - m_new)
        l = alpha * l + jnp.sum(p, axis=-1, keepdims=True)
        acc = alpha * acc + jnp.dot(p.astype(BF16), vb, preferred_element_type=F32)
        return m_new, l, acc

    init = (jnp.full((blk, 1), -jnp.inf, F32), jnp.zeros((blk, 1), F32), jnp.zeros((blk, HEAD_DIM), F32))
    carry = attend(qi, init, r_io >= c_io)

    def past(n, carry):
        chosen = jnp.sum(jnp.where(col == n, sel, 0.0), axis=-1, keepdims=True) > 0.5
        return attend(n, carry, chosen)

    m, l, acc = lax.fori_loop(0, qi, past, carry)
    o_ref[...] = (acc / l).astype(o_ref.dtype)


def moba_prompt(q, k, v, *, batch, seq):
    m, d = q.shape
    nb = seq // MOBA_BLOCK
    return pl.pallas_call(
        functools.partial(_moba_kernel, nb=nb),
        grid=(batch, N_HEADS, nb),
        in_specs=[
            pl.BlockSpec((MOBA_BLOCK, HEAD_DIM), lambda b, h, i: (b * nb + i, h)),
            pl.BlockSpec((seq, HEAD_DIM), lambda b, h, i: (b, h)),
            pl.BlockSpec((seq, HEAD_DIM), lambda b, h, i: (b, h)),
        ],
        out_specs=pl.BlockSpec((MOBA_BLOCK, HEAD_DIM), lambda b, h, i: (b * nb + i, h)),
        out_shape=jax.ShapeDtypeStruct((m, d), BF16),
        scratch_shapes=[pltpu.VMEM((nb, HEAD_DIM), F32)],
        compiler_params=_cparams("parallel", "parallel", "arbitrary"),
        name="moba_prompt",
    )(q, k, v)


def _decode_gate_kernel(pt_ref, k0_ref, k1_ref, q_ref, idx_ref, gate_ref, *, n_blocks):
    del pt_ref
    n = pl.program_id(1)
    lane = lax.broadcasted_iota(jnp.int32, gate_ref.shape, 1)

    @pl.when(n == 0)
    def _():
        gate_ref[...] = jnp.full(gate_ref.shape, -jnp.inf, F32)

    k_mean = (jnp.sum(k0_ref[...], axis=0) + jnp.sum(k1_ref[...], axis=0)) * (1.0 / MOBA_BLOCK)
    score = jnp.sum(k_mean * q_ref[...], axis=-1, keepdims=True)
    gate_ref[...] = jnp.where(lane == n, score, gate_ref[...])

    @pl.when(n == n_blocks - 1)
    def _():
        picks = _topk_lanes(gate_ref[...], lane, LANES, MOBA_TOPK)
        out = jnp.zeros(idx_ref.shape, jnp.int32)
        for rank, idx in enumerate(picks):
            out = jnp.where(lane == rank, idx, out)
        idx_ref[...] = out


def decode_gate(q, cache_k, page_table, ia):
    bsz, d = q.shape
    n_pages = page_table.shape[1]
    pages_per_block = MOBA_BLOCK // PAGE_SIZE
    assert pages_per_block == 2
    n_blocks = n_pages // pages_per_block
    assert MOBA_TOPK <= n_blocks <= LANES
    page_spec = lambda half: pl.BlockSpec(
        (None, None, PAGE_SIZE, N_HEADS, HEAD_DIM), lambda b, n, pt: (ia, pt[b, 2 * n + half], 0, 0, 0))
    grid_spec = pltpu.PrefetchScalarGridSpec(
        num_scalar_prefetch=1,
        grid=(bsz, n_blocks),
        in_specs=[page_spec(0), page_spec(1),
                  pl.BlockSpec((None, N_HEADS, HEAD_DIM), lambda b, n, pt: (b, 0, 0))],
        out_specs=pl.BlockSpec((None, N_HEADS, LANES), lambda b, n, pt: (b, 0, 0)),
        scratch_shapes=[pltpu.VMEM((N_HEADS, LANES), F32)],
    )
    idx = pl.pallas_call(
        functools.partial(_decode_gate_kernel, n_blocks=n_blocks),
        grid_spec=grid_spec,
        out_shape=jax.ShapeDtypeStruct((bsz, N_HEADS, LANES), jnp.int32),
        compiler_params=_cparams("parallel", "arbitrary"),
        name="decode_gate",
    )(page_table, cache_k, cache_k, q.reshape(bsz, N_HEADS, HEAD_DIM))
    return idx[:, :, :MOBA_TOPK].reshape(bsz, N_HEADS * MOBA_TOPK)


def _decode_attend_kernel(pt_ref, idx_ref, q_ref, kn_ref, vn_ref, kp_ref, vp_ref, o_ref,
                          m_ref, l_ref, acc_ref, *, past_len):
    del pt_ref
    b = pl.program_id(0)
    h = pl.program_id(1)
    j = pl.program_id(2)
    scale = HEAD_DIM ** -0.5
    q = q_ref[...]

    @pl.when(j == 0)
    def _():
        m_ref[...] = jnp.sum(q * kn_ref[...], axis=-1, keepdims=True) * scale
        l_ref[...] = jnp.ones_like(l_ref)
        acc_ref[...] = vn_ref[...]

    rows = PAGE_SIZE * HEAD_GROUP
    kp = kp_ref[...].reshape(rows, HEAD_DIM).astype(BF16)
    vp = vp_ref[...].reshape(rows, HEAD_DIM).astype(BF16)
    blk_id = idx_ref[b, h * MOBA_TOPK + j // 2]
    first_pos = blk_id * MOBA_BLOCK + (j % 2) * PAGE_SIZE
    col = lax.broadcasted_iota(jnp.int32, (1, rows), 1)
    pos = first_pos + col // HEAD_GROUP
    mine = col % HEAD_GROUP == h % HEAD_GROUP
    dist = (past_len - pos).astype(F32)
    s = lax.dot_general(q.astype(BF16), kp, NT_DIMS, preferred_element_type=F32) * scale
    s = jnp.where(mine, s - _alibi_slope(h) * dist, -jnp.inf)
    m_old = m_ref[...]
    m_new = jnp.maximum(m_old, jnp.max(s, axis=-1, keepdims=True))
    alpha = jnp.exp(m_old - m_new)
    p = jnp.exp(s - m_new)
    l_ref[...] = alpha * l_ref[...] + jnp.sum(p, axis=-1, keepdims=True)
    acc_ref[...] = alpha * acc_ref[...] + jnp.dot(p.astype(BF16), vp, preferred_element_type=F32)
    m_ref[...] = m_new

    @pl.when(j == pl.num_programs(2) - 1)
    def _():
        o_ref[...] = (acc_ref[...] / l_ref[...]).astype(o_ref.dtype)


def decode_attend(q, k_new, v_new, cache_k, cache_v, page_table, top_idx, ia):
    bsz, d = q.shape
    past_len = page_table.shape[1] * PAGE_SIZE
    n_steps = MOBA_TOPK * (MOBA_BLOCK // PAGE_SIZE)
    tok_spec = pl.BlockSpec((None, 1, HEAD_DIM), lambda b, h, j, pt, ix: (b, 0, h))

    def page_map(b, h, j, pt, ix):
        return (ia, pt[b, 2 * ix[b, h * MOBA_TOPK + j // 2] + j % 2], 0, h // HEAD_GROUP, 0)

    page_spec = pl.BlockSpec((None, None, PAGE_SIZE, HEAD_GROUP, HEAD_DIM), page_map)
    grid_spec = pltpu.PrefetchScalarGridSpec(
        num_scalar_prefetch=2,
        grid=(bsz, N_HEADS, n_steps),
        in_specs=[tok_spec, tok_spec, tok_spec, page_spec, page_spec],
        out_specs=tok_spec,
        scratch_shapes=[pltpu.VMEM((1, 1), F32), pltpu.VMEM((1, 1), F32), pltpu.VMEM((1, HEAD_DIM), F32)],
    )
    tok = lambda z: z.reshape(bsz, 1, d)
    out = pl.pallas_call(
        functools.partial(_decode_attend_kernel, past_len=past_len),
        grid_spec=grid_spec,
        out_shape=jax.ShapeDtypeStruct((bsz, 1, d), BF16),
        compiler_params=_cparams("parallel", "parallel", "arbitrary"),
        name="decode_attend",
    )(page_table, top_idx, tok(q), tok(k_new), tok(v_new), cache_k, cache_v)
    return out.reshape(bsz, d)


def _mix_kernel(x_ref, g_ref, sp_ref, mix_ref, xm_ref, last_ref, prev_ref):
    tm = x_ref.shape[0]

    @pl.when(pl.program_id(1) == 0)
    def _():
        prev_ref[...] = sp_ref[...]

    xn = _rms(x_ref[...], g_ref[...])
    if tm == 1:
        xp = prev_ref[...]
    else:
        row = lax.broadcasted_iota(jnp.int32, xn.shape, 0)
        xp = jnp.where(row == 0, prev_ref[...], pltpu.roll(xn, 1, axis=0))
    delta = xp - xn
    for s in range(xm_ref.shape[0]):
        xm_ref[s] = (xn + delta * mix_ref[s:s + 1, :]).astype(BF16)
    prev_ref[...] = xn[tm - 1:tm, :]
    last_ref[...] = xn[tm - 1:tm, :]


def rwkv_mix(x, g, shift_prev, mix, *, batch, seq, tm_pref=256):
    d = x.shape[-1]
    n_mix = mix.shape[0]
    tm = _pick_tile(seq, tm_pref)
    xm, last = pl.pallas_call(
        _mix_kernel,
        grid=(batch, seq // tm),
        in_specs=[
            pl.BlockSpec((None, tm, d), lambda b, t: (b, t, 0)),
            pl.BlockSpec((1, d), lambda b, t: (0, 0)),
            pl.BlockSpec((None, 1, d), lambda b, t: (b, 0, 0)),
            pl.BlockSpec((n_mix, d), lambda b, t: (0, 0)),
        ],
        out_specs=[
            pl.BlockSpec((n_mix, None, tm, d), lambda b, t: (0, b, t, 0)),
            pl.BlockSpec((None, 1, d), lambda b, t: (b, 0, 0)),
        ],
        out_shape=[
            jax.ShapeDtypeStruct((n_mix, batch, seq, d), BF16),
            jax.ShapeDtypeStruct((batch, 1, d), F32),
        ],
        scratch_shapes=[pltpu.VMEM((1, d), F32)],
        compiler_params=_cparams("parallel", "arbitrary"),
        name="rwkv_mix",
    )(x.reshape(batch, seq, d), g.reshape(1, d), shift_prev.reshape(batch, 1, d), mix)
    return xm.reshape(n_mix, batch * seq, d), last.reshape(batch, d)


def _bmm_kernel(x_ref, w_ref, o_ref):
    o_ref[...] = jnp.dot(x_ref[...], w_ref[...], preferred_element_type=F32)


def rkv_proj(xm, w, *, tm_pref=1024, tn=512):
    _, m, d = xm.shape
    ns, _, n = w.shape
    tm = _pick_tile(m, tm_pref)
    return pl.pallas_call(
        _bmm_kernel,
        grid=(ns, m // tm, n // tn),
        in_specs=[
            pl.BlockSpec((None, tm, d), lambda s, i, j: (s, i, 0)),
            pl.BlockSpec((None, d, tn), lambda s, i, j: (s, 0, j)),
        ],
        out_specs=pl.BlockSpec((None, tm, tn), lambda s, i, j: (s, i, j)),
        out_shape=jax.ShapeDtypeStruct((ns, m, n), F32),
        compiler_params=_cparams("parallel", "parallel", "arbitrary"),
        name="rkv_proj",
    )(xm, w)


def _lora_kernel(x_ref, w1_ref, w2_ref, o_ref, *, act):
    hid = jnp.dot(x_ref[...], w1_ref[...], preferred_element_type=F32)
    if act == "tanh":
        hid = jnp.tanh(hid)
    elif act == "sigmoid":
        hid = jax.nn.sigmoid(hid)
    o_ref[...] = jnp.dot(hid.astype(BF16), w2_ref[...], preferred_element_type=F32)


def lora(xm, stream, w1, w2, act, *, tm_pref=512):
    _, m, d = xm.shape
    r = w1.shape[1]
    tm = _pick_tile(m, tm_pref)
    return pl.pallas_call(
        functools.partial(_lora_kernel, act=act),
        grid=(m // tm,),
        in_specs=[
            pl.BlockSpec((None, tm, d), lambda i: (stream, i, 0)),
            pl.BlockSpec((d, r), lambda i: (0, 0)),
            pl.BlockSpec((r, d), lambda i: (0, 0)),
        ],
        out_specs=pl.BlockSpec((tm, d), lambda i: (i, 0)),
        out_shape=jax.ShapeDtypeStruct((m, d), F32),
        compiler_params=_cparams("parallel"),
        name="lora_" + act,
    )(xm, w1, w2)


def _hdot(a, b):
    return jnp.dot(a, b, precision=HIGHEST, preferred_element_type=F32)


def _wkv_kernel(*refs, chunk, t_valid, use_vres):
    if use_vres:
        (r_ref, k_ref, v_ref, wl_ref, al_ref, g_ref, vl_ref, vf_ref,
         w0_ref, a0_ref, kk_ref, ka_ref, rk_ref, lnw_ref, lnb_ref, v0_ref,
         s0_ref, y_ref, so_ref, st_ref) = refs
    else:
        (r_ref, k_ref, v_ref, wl_ref, al_ref, g_ref,
         w0_ref, a0_ref, kk_ref, ka_ref, rk_ref, lnw_ref, lnb_ref,
         s0_ref, y_ref, so_ref, st_ref) = refs
    tb = pl.program_id(2)
    c = chunk
    hd = RWKV_HEAD_DIM
    heads = r_ref.shape[1] // hd

    @pl.when(tb == 0)
    def _():
        st_ref[...] = s0_ref[...]

    row = lax.broadcasted_iota(jnp.int32, (c, c), 0)
    colc = lax.broadcasted_iota(jnp.int32, (c, c), 1)
    lower = row >= colc
    strict = row > colc
    ones_lower = lower.astype(F32)
    n_doublings = int(math.log2(c))

    for ci in range(r_ref.shape[0] // c):
        sl = slice(ci * c, (ci + 1) * c)
        r2 = r_ref[sl, :]
        k2 = k_ref[sl, :]
        v2 = v_ref[sl, :]
        w_log = -jax.nn.softplus(-(w0_ref[...] + wl_ref[sl, :])) - 0.5
        lw2 = -jnp.exp(w_log)
        asig2 = jax.nn.sigmoid(a0_ref[...] + al_ref[sl, :])
        if use_vres:
            v2 = v2 + (vf_ref[sl, :] - v2) * jax.nn.sigmoid(v0_ref[...] + vl_ref[sl, :])
        kkraw2 = k2 * kk_ref[...]
        kmod2 = k2 * (1.0 + (asig2 - 1.0) * ka_ref[...])
        bonus2 = r2 * kmod2 * rk_ref[...]
        if t_valid is not None:
            live = lax.broadcasted_iota(jnp.int32, r2.shape, 0) + ci * c < t_valid
            lw2 = jnp.where(live, lw2, 0.0)
            kkraw2 = jnp.where(live, kkraw2, 0.0)
            kmod2 = jnp.where(live, kmod2, 0.0)
            v2 = jnp.where(live, v2, 0.0)

        outs = []
        for hh in range(heads):
            ln = slice(hh * hd, (hh + 1) * hd)
            r, v, lw, asig, kmod = r2[:, ln], v2[:, ln], lw2[:, ln], asig2[:, ln], kmod2[:, ln]
            kk = kkraw2[:, ln]
            kk = kk / jnp.maximum(jnp.sqrt(jnp.sum(kk * kk, axis=-1, keepdims=True)), 1e-12)
            cum = _hdot(ones_lower, lw)
            e_pos = jnp.exp(cum)
            e_neg = jnp.exp(-cum)
            w_end = e_pos[c - 1:c, :]
            a_t = -kk * jnp.exp(cum - lw)
            r_t = r * e_pos
            b_t = kk * asig * e_neg
            k_t = kmod * e_neg
            x_ar = jnp.concatenate([a_t, r_t], axis=0)
            z_bk = jnp.concatenate([b_t, k_t], axis=0)
            gram = lax.dot_general(x_ar, z_bk, NT_DIMS, precision=HIGHEST, preferred_element_type=F32)
            a_ab = jnp.where(strict, gram[:c, :c], 0.0)
            a_ak = jnp.where(strict, gram[:c, c:], 0.0)
            a_rb = jnp.where(lower, gram[c:, :c], 0.0)
            a_rk = jnp.where(lower, gram[c:, c:], 0.0)
            s_prev = st_ref[hh]
            xs = lax.dot_general(x_ar, s_prev, NT_DIMS, precision=HIGHEST, preferred_element_type=F32)
            u = xs[:c] + _hdot(a_ak, v)
            npow = a_ab
            u = u + _hdot(npow, u)
            for _ in range(n_doublings - 1):
                npow = _hdot(npow, npow)
                u = u + _hdot(npow, u)
            uv = jnp.concatenate([u, v], axis=0)
            y = xs[c:] + _hdot(jnp.concatenate([a_rb, a_rk], axis=1), uv)
            st_ref[hh] = s_prev * w_end + lax.dot_general(
                uv, z_bk * w_end, TN_DIMS, precision=HIGHEST, preferred_element_type=F32)

            mu = jnp.mean(y, axis=-1, keepdims=True)
            yc = y - mu
            var = jnp.mean(yc * yc, axis=-1, keepdims=True)
            yn = yc * lax.rsqrt(var + GN_EPS) * lnw_ref[:, ln] + lnb_ref[:, ln]
            outs.append(yn + jnp.sum(bonus2[:, ln], axis=-1, keepdims=True) * v)
        y_ref[sl, :] = (jnp.concatenate(outs, axis=1) * g_ref[sl, :]).astype(y_ref.dtype)

    @pl.when(tb == pl.num_programs(2) - 1)
    def _():
        so_ref[...] = st_ref[...]


def wkv(rkv, wl, al, gate, s0, params, vres, *, batch, seq, chunk, tb_rows, t_valid=None):
    _, m, d = rkv.shape
    hd = RWKV_HEAD_DIM
    lanes = LANES
    heads_per = lanes // hd
    n_hp = d // lanes
    n_tb = seq // tb_rows
    use_vres = vres is not None
    row_spec = pl.BlockSpec((tb_rows, lanes), lambda b, p, t: (b * n_tb + t, p))
    rkv_spec = lambda s: pl.BlockSpec((None, tb_rows, lanes), lambda b, p, t: (s, b * n_tb + t, p))
    par_spec = pl.BlockSpec((1, lanes), lambda b, p, t: (0, p))
    st_spec = pl.BlockSpec((None, heads_per, hd, hd), lambda b, p, t: (b, p, 0, 0))
    par = lambda z: z.reshape(1, d)
    ins = [rkv, rkv, rkv, wl, al, gate]
    specs = [rkv_spec(0), rkv_spec(1), rkv_spec(2), row_spec, row_spec, row_spec]
    if use_vres:
        vl, v_first, v0 = vres
        ins += [vl, v_first]
        specs += [row_spec, row_spec]
    ins += [par(params[n]) for n in ("w0", "a0", "k_k", "k_a", "r_k", "ln_w", "ln_b")]
    specs += [par_spec] * 7
    if use_vres:
        ins.append(par(v0))
        specs.append(par_spec)
    ins.append(s0)
    specs.append(st_spec)
    y, s_out = pl.pallas_call(
        functools.partial(_wkv_kernel, chunk=chunk, t_valid=t_valid, use_vres=use_vres),
        grid=(batch, n_hp, n_tb),
        in_specs=specs,
        out_specs=[row_spec, st_spec],
        out_shape=[jax.ShapeDtypeStruct((m, d), BF16), jax.ShapeDtypeStruct(s0.shape, F32)],
        scratch_shapes=[pltpu.VMEM((heads_per, hd, hd), F32)],
        compiler_params=_cparams("parallel", "parallel", "arbitrary"),
        name="wkv",
    )(*ins)
    return y, s_out


WKV_CHUNK = 64


def _rwkv_layer(x, g, w, ir, shift_prev, s0, v_first, *, batch, seq):
    m, d = x.shape
    xm, last = rwkv_mix(x, g, shift_prev, w["rwkv_mix"][ir], batch=batch, seq=seq)
    rkv = rkv_proj(xm, w["rwkv_w_rkv"][ir])
    wl = lora(xm, 3, w["rwkv_w1"][ir], w["rwkv_w2"][ir], "tanh")
    al = lora(xm, 4, w["rwkv_a1"][ir], w["rwkv_a2"][ir], "none")
    gate = lora(xm, 5, w["rwkv_g1"][ir], w["rwkv_g2"][ir], "sigmoid")
    if ir == 0:
        vres = None
        v_first = rkv[2]
    else:
        vl = lora(xm, 2, w["vres_v1"][ir - 1], w["vres_v2"][ir - 1], "none")
        vres = (vl, v_first, w["vres_v0"][ir - 1])
    params = {n: w["rwkv_" + n][ir] for n in ("w0", "a0", "k_k", "k_a", "r_k", "ln_w", "ln_b")}
    if seq % WKV_CHUNK == 0:
        tb_rows = _pick_tile(seq, 256)
        y, s_new = wkv(rkv, wl, al, gate, s0, params, vres, batch=batch, seq=seq,
                       chunk=WKV_CHUNK, tb_rows=tb_rows)
    else:
        pad_to = -(-seq // WKV_CHUNK) * WKV_CHUNK

        def padrows(z):
            z = z.reshape(z.shape[:-2] + (batch, seq, d))
            cfg = [(0, 0)] * (z.ndim - 2) + [(0, pad_to - seq), (0, 0)]
            return jnp.pad(z, cfg).reshape(z.shape[:-3] + (batch * pad_to, d))

        vres_p = None if vres is None else (padrows(vres[0]), padrows(vres[1]), vres[2])
        y, s_new = wkv(padrows(rkv), padrows(wl), padrows(al), padrows(gate), s0, params, vres_p,
                       batch=batch, seq=pad_to, chunk=WKV_CHUNK, tb_rows=pad_to, t_valid=seq)
        y = y.reshape(batch, pad_to, d)[:, :seq].reshape(m, d)
    return mm_res(y, w["rwkv_w_o"][ir], x), v_first, s_new, last


def _trunk(x, p, w, *, batch, seq, attend, wkv0, shift0):
    depth = p.shape[0]
    new_k, new_v, new_wkv, new_shift = [], [], [], []
    v_first = None
    ia = ir = 0
    for i in range(depth):
        g = w["norm_g"][i]
        x = ffn(x, g[0], w["ffa_wi"][i], w["ffa_wo"][i])
        if i % 2 == 0:
            q, k, v = qkv_proj(x, g[1], w["attn_w_qkv"][ia], w["attn_q_g"][ia], w["attn_k_g"][ia])
            o = attend(ia, q, k, v)
            x = mm_res(o, w["attn_w_o"][ia], x)
            new_k.append(k)
            new_v.append(v)
            ia += 1
        else:
            x, v_first, s_new, last = _rwkv_layer(x, g[1], w, ir, shift0[ir], wkv0[ir], v_first,
                                                  batch=batch, seq=seq)
            new_wkv.append(s_new)
            new_shift.append(last)
            ir += 1
        x = ffn(x, g[2], w["ffb_wi"][i], w["ffb_wo"][i])
        x = ple(x, g[3], w["ple_w_gate"][i], p[i], w["ple_w_in"][i])
    return x, jnp.stack(new_k), jnp.stack(new_v), jnp.stack(new_wkv), jnp.stack(new_shift)


_MATMUL_WEIGHTS = ("ffa_wi", "ffa_wo", "ffb_wi", "ffb_wo", "attn_w_qkv", "attn_w_o", "rwkv_w_rkv",
                   "rwkv_w1", "rwkv_w2", "rwkv_a1", "rwkv_a2", "rwkv_g1", "rwkv_g2", "rwkv_w_o",
                   "vres_v1", "vres_v2", "ple_w_in", "ple_w_gate")


def kernel(x_prompt, x_sample, cache_k, cache_v, page_table, state_wkv, state_shift, p_prompt, p_sample, norm_g, ffa_wi, ffa_wo, ffb_wi, ffb_wo, attn_w_qkv, attn_q_g, attn_k_g, attn_w_o, rwkv_mix, rwkv_w_rkv, rwkv_w0, rwkv_w1, rwkv_w2, rwkv_a0, rwkv_a1, rwkv_a2, rwkv_g1, rwkv_g2, rwkv_k_k, rwkv_k_a, rwkv_r_k, rwkv_ln_w, rwkv_ln_b, rwkv_w_o, vres_v0, vres_v1, vres_v2, ple_w_in, ple_w_gate):
    w = dict(norm_g=norm_g, ffa_wi=ffa_wi, ffa_wo=ffa_wo, ffb_wi=ffb_wi, ffb_wo=ffb_wo,
             attn_w_qkv=attn_w_qkv, attn_q_g=attn_q_g, attn_k_g=attn_k_g, attn_w_o=attn_w_o,
             rwkv_mix=rwkv_mix, rwkv_w_rkv=rwkv_w_rkv, rwkv_w0=rwkv_w0, rwkv_w1=rwkv_w1, rwkv_w2=rwkv_w2,
             rwkv_a0=rwkv_a0, rwkv_a1=rwkv_a1, rwkv_a2=rwkv_a2, rwkv_g1=rwkv_g1, rwkv_g2=rwkv_g2,
             rwkv_k_k=rwkv_k_k, rwkv_k_a=rwkv_k_a, rwkv_r_k=rwkv_r_k, rwkv_ln_w=rwkv_ln_w,
             rwkv_ln_b=rwkv_ln_b, rwkv_w_o=rwkv_w_o, vres_v0=vres_v0, vres_v1=vres_v1, vres_v2=vres_v2,
             ple_w_in=ple_w_in, ple_w_gate=ple_w_gate)
    for name in _MATMUL_WEIGHTS:
        w[name] = w[name].astype(BF16)

    bp, tp, d = x_prompt.shape
    bs, ts, _ = x_sample.shape
    n_rwkv = state_wkv.shape[0]
    n_attn = cache_k.shape[0]
    n_pool = cache_k.shape[1]
    dp = p_prompt.shape[-1]
    hd_r = RWKV_HEAD_DIM

    wkv0 = jnp.zeros((n_rwkv, bp, d // hd_r, hd_r, hd_r), F32)
    shift0 = jnp.zeros((n_rwkv, bp, d), F32)
    attend_p = lambda ia, q, k, v: moba_prompt(q, k, v, batch=bp, seq=tp)
    y_p, k_p, v_p, wkv_p, sh_p = _trunk(
        x_prompt.reshape(bp * tp, d), p_prompt.reshape(-1, bp * tp, dp), w,
        batch=bp, seq=tp, attend=attend_p, wkv0=wkv0, shift0=shift0)

    def attend_s(ia, q, k, v):
        top_idx = decode_gate(q, cache_k, page_table, ia)
        return decode_attend(q, k, v, cache_k, cache_v, page_table, top_idx, ia)

    y_s, k_s, v_s, wkv_s, sh_s = _trunk(
        x_sample.reshape(bs * ts, d), p_sample.reshape(-1, bs * ts, dp), w,
        batch=bs, seq=ts, attend=attend_s, wkv0=state_wkv, shift0=state_shift)

    heads = lambda z, b, t: z.reshape(z.shape[0], b, t, N_HEADS, HEAD_DIM)
    return (y_p.reshape(bp, tp, d), y_s.reshape(bs, ts, d),
            heads(k_p, bp, tp), heads(v_p, bp, tp), heads(k_s, bs, ts), heads(v_s, bs, ts),
            wkv_p, sh_p, wkv_s, sh_s)
```
